```python
import math
import jax, jax.numpy as jnp
from jax import lax
import numpy as np

D_MODEL = 1024
BATCH = 8
SEQ = 2048
DEPTH = 1

SSM_GROUP = 16
SSM_GROUPS = 32
SSM_WIDTH = SSM_GROUP * SSM_GROUPS
SSM_STATE = 64
SSM_DT_MIN = 1e-3
SSM_DT_MAX = 1e-1
ATT_HEADS = 8
HEAD_DIM = 64
ATT_WIDTH = ATT_HEADS * HEAD_DIM
MOBA_BLOCK = 256
MOBA_TOPK = 3
ATT_Q_CHUNK = 32
REL_BUCKETS = 32
REL_MAX_DIST = 128
PEER_HEADS = 8
PEER_NKEYS = 128
PEER_EXPERTS = PEER_NKEYS * PEER_NKEYS
PEER_QDIM = 256
PEER_TOPK = 16
PEER_TOK_CHUNK = 128
RMS_EPS = 1e-6
NEG = -1e30
IN_WIDTHS = (SSM_WIDTH, ATT_WIDTH, ATT_WIDTH, ATT_WIDTH, D_MODEL, D_MODEL)
IN_COLS = sum(IN_WIDTHS)
IN_SPLITS = tuple(int(s) for s in np.cumsum(IN_WIDTHS)[:-1])

kernel_name = "hybrid_s5_moba_peer_block"


def rms_norm(x, g):
    xf = x.astype(jnp.float32)
    y = xf * lax.rsqrt(jnp.mean(xf * xf, axis=-1, keepdims=True) + RMS_EPS)
    return (y * g.astype(jnp.float32)).astype(x.dtype)


def _ssm_combine(c1, c2):
    a1, b1 = c1
    a2, b2 = c2
    return a1 * a2, a2 * b1 + b2


def s5_branch(u, a_re, a_im, log_dt, b_re, b_im, c_re, c_im, d_skip, w_glu, b_glu):
    f32 = jnp.float32
    bsz, seq, _ = u.shape
    uf = u.astype(f32).reshape(bsz, seq, SSM_GROUPS, SSM_GROUP)
    lam = lax.complex(a_re.astype(f32), a_im.astype(f32))
    dt = jnp.exp(log_dt.astype(f32))[:, None]
    lam_bar = jnp.exp(lam * dt)
    bmat = lax.complex(b_re.astype(f32), b_im.astype(f32))
    b_bar = ((lam_bar - 1.0) / lam)[..., None] * bmat
    cmat = lax.complex(c_re.astype(f32), c_im.astype(f32))
    bu = jnp.einsum('gph,blgh->blgp', b_bar, uf)
    a = jnp.broadcast_to(lam_bar, bu.shape)
    _, states = lax.associative_scan(_ssm_combine, (a, bu), axis=1)
    y = jnp.einsum('ghp,blgp->blgh', cmat, states).real + d_skip.astype(f32) * uf
    y = jax.nn.gelu(y.reshape(bsz, seq, SSM_WIDTH)).astype(u.dtype)
    return y * jax.nn.sigmoid(y @ w_glu + b_glu)


def t5_bucket(rel):
    n = jnp.maximum(rel, 0)
    max_exact = REL_BUCKETS // 2
    nf = jnp.maximum(n, 1).astype(jnp.float32)
    large = max_exact + (jnp.log(nf / max_exact) / math.log(REL_MAX_DIST / max_exact)
                         * (REL_BUCKETS - max_exact)).astype(jnp.int32)
    large = jnp.clip(large, 0, REL_BUCKETS - 1)
    return jnp.where(n < max_exact, n, large)


def moba_branch(q, k, v, rel_bias):
    f32 = jnp.float32
    bsz, seq, _ = q.shape
    nb = -(-seq // MOBA_BLOCK)
    lp = nb * MOBA_BLOCK

    def heads(t):
        t = t.reshape(bsz, seq, ATT_HEADS, HEAD_DIM).transpose(0, 2, 1, 3)
        return jnp.pad(t, ((0, 0), (0, 0), (0, lp - seq), (0, 0)))

    qh, kh, vh = heads(q), heads(k), heads(v)
    kb = kh.reshape(bsz, ATT_HEADS, nb, MOBA_BLOCK, HEAD_DIM)
    vb = vh.reshape(bsz, ATT_HEADS, nb, MOBA_BLOCK, HEAD_DIM)

    kmean = jnp.mean(kb.astype(f32), axis=3)
    gate = jnp.einsum('bhqd,bhnd->bhqn', qh.astype(f32), kmean)
    qblk = jnp.arange(lp) // MOBA_BLOCK
    past = jnp.arange(nb)[None, :] < qblk[:, None]
    gate = jnp.where(past, gate, -jnp.inf)
    topk = min(MOBA_TOPK, nb)
    _, sel = lax.top_k(gate, topk)

    n_chunks = lp // ATT_Q_CHUNK
    qc = qh.reshape(bsz, ATT_HEADS, n_chunks, ATT_Q_CHUNK, HEAD_DIM).transpose(2, 0, 1, 3, 4)
    selc = sel.reshape(bsz, ATT_HEADS, n_chunks, ATT_Q_CHUNK, topk).transpose(2, 0, 1, 3, 4)
    starts = jnp.arange(n_chunks, dtype=jnp.int32) * ATT_Q_CHUNK

    bi = jnp.arange(bsz)[:, None, None, None]
    hi = jnp.arange(ATT_HEADS)[None, :, None, None]
    hi5 = hi[..., None]
    table = rel_bias.astype(f32).T
    offs = jnp.arange(MOBA_BLOCK)
    scale = HEAD_DIM ** -0.5

    def step(args):
        q_c, sel_c, start = args
        qp = start + jnp.arange(ATT_Q_CHUNK)
        own = start // MOBA_BLOCK
        q_f = q_c.astype(f32)
        k_sel = kb[bi, hi, sel_c].astype(f32)
        v_sel = vb[bi, hi, sel_c].astype(f32)
        kpos_sel = sel_c[..., None] * MOBA_BLOCK + offs
        bias_sel = table[hi5, t5_bucket(qp[:, None, None] - kpos_sel)]
        s_sel = jnp.einsum('bhqd,bhqskd->bhqsk', q_f, k_sel) * scale + bias_sel
        s_sel = jnp.where((sel_c < own)[..., None], s_sel, NEG)
        k_own = lax.dynamic_index_in_dim(kb, own, axis=2, keepdims=False).astype(f32)
        v_own = lax.dynamic_index_in_dim(vb, own, axis=2, keepdims=False).astype(f32)
        rel_own = qp[:, None] - (own * MOBA_BLOCK + offs)[None, :]
        s_own = jnp.einsum('bhqd,bhkd->bhqk', q_f, k_own) * scale + table[:, t5_bucket(rel_own)]
        s_own = jnp.where(rel_own >= 0, s_own, NEG)
        s = jnp.concatenate([s_sel.reshape(bsz, ATT_HEADS, ATT_Q_CHUNK, topk * MOBA_BLOCK), s_own], axis=-1)
        p = jax.nn.softmax(s, axis=-1)
        p_sel = p[..., :topk * MOBA_BLOCK].reshape(bsz, ATT_HEADS, ATT_Q_CHUNK, topk, MOBA_BLOCK)
        p_own = p[..., topk * MOBA_BLOCK:]
        o = (jnp.einsum('bhqsk,bhqskd->bhqd', p_sel, v_sel)
             + jnp.einsum('bhqk,bhkd->bhqd', p_own, v_own))
        return o.astype(q_c.dtype)

    out = lax.map(step, (qc, selc, starts))
    out = out.transpose(1, 2, 0, 3, 4).reshape(bsz, ATT_HEADS, lp, HEAD_DIM)[:, :, :seq]
    return out.transpose(0, 2, 1, 3).reshape(bsz, seq, ATT_WIDTH)


def peer_ffn(h, w_q, sub_k1, sub_k2, u_emb, v_emb):
    f32 = jnp.float32
    bsz, seq, d = h.shape
    t = h.reshape(-1, d)
    n_tok = t.shape[0]
    half = PEER_QDIM // 2
    q = (t @ w_q).astype(f32).reshape(n_tok, PEER_HEADS, 2, half)
    s1 = jnp.einsum('thd,kd->thk', q[:, :, 0], sub_k1.astype(f32))
    s2 = jnp.einsum('thd,kd->thk', q[:, :, 1], sub_k2.astype(f32))
    v1, i1 = lax.top_k(s1, PEER_TOPK)
    v2, i2 = lax.top_k(s2, PEER_TOPK)
    n_cand = PEER_TOPK * PEER_TOPK
    cand = (v1[..., :, None] + v2[..., None, :]).reshape(n_tok, PEER_HEADS, n_cand)
    cand_idx = (i1[..., :, None] * PEER_NKEYS + i2[..., None, :]).reshape(n_tok, PEER_HEADS, n_cand)
    best, pos = lax.top_k(cand, PEER_TOPK)
    expert = jnp.take_along_axis(cand_idx, pos, axis=-1)
    gate = jax.nn.softmax(best, axis=-1)

    n_chunks = n_tok // PEER_TOK_CHUNK

    def step(args):
        t_c, e_c, g_c = args
        u_sel = jnp.take(u_emb, e_c, axis=0).astype(f32)
        act = jax.nn.gelu(jnp.einsum('cd,chkd->chk', t_c.astype(f32), u_sel))
        v_sel = jnp.take(v_emb, e_c, axis=0).astype(f32)
        return jnp.einsum('chk,chkd->cd', g_c * act, v_sel).astype(t_c.dtype)

    y = lax.map(step, (t.reshape(n_chunks, PEER_TOK_CHUNK, d),
                       expert.reshape(n_chunks, PEER_TOK_CHUNK, PEER_HEADS, PEER_TOPK),
                       gate.reshape(n_chunks, PEER_TOK_CHUNK, PEER_HEADS, PEER_TOPK)))
    return y.reshape(bsz, seq, d)


def setup_inputs(seed: int = 0) -> dict:
    key = jax.random.key(seed)
    ks = jax.random.split(key, 26)
    f32 = jnp.float32

    def nrm(k, shape, s):
        return jax.random.normal(k, shape, f32) * s

    G, P, H = SSM_GROUPS, SSM_STATE, SSM_GROUP
    n_idx = jnp.arange(SSM_STATE, dtype=f32)
    return {
        "x": nrm(ks[0], (BATCH, SEQ, D_MODEL), 1.0),
        "norm1_g": 1.0 + nrm(ks[1], (DEPTH, D_MODEL), 0.05),
        "w_in": nrm(ks[2], (DEPTH, D_MODEL, IN_COLS), D_MODEL ** -0.5),
        "ssm_a_re": -0.5 + nrm(ks[3], (DEPTH, G, P), 0.01),
        "ssm_a_im": math.pi * n_idx + nrm(ks[4], (DEPTH, G, P), 0.01),
        "ssm_log_dt": jax.random.uniform(ks[5], (DEPTH, G), f32, math.log(SSM_DT_MIN), math.log(SSM_DT_MAX)),
        "ssm_b_re": nrm(ks[6], (DEPTH, G, P, H), (2 * H) ** -0.5),
        "ssm_b_im": nrm(ks[7], (DEPTH, G, P, H), (2 * H) ** -0.5),
        "ssm_c_re": nrm(ks[8], (DEPTH, G, H, P), (2 * P) ** -0.5),
        "ssm_c_im": nrm(ks[9], (DEPTH, G, H, P), (2 * P) ** -0.5),
        "ssm_d": nrm(ks[10], (DEPTH, G, H), 1.0),
        "ssm_w_glu": nrm(ks[11], (DEPTH, SSM_WIDTH, SSM_WIDTH), SSM_WIDTH ** -0.5),
        "ssm_b_glu": nrm(ks[12], (DEPTH, SSM_WIDTH), 0.01),
        "w_up_ssm": nrm(ks[13], (DEPTH, SSM_WIDTH, D_MODEL), SSM_WIDTH ** -0.5),
        "w_up_att": nrm(ks[14], (DEPTH, ATT_WIDTH, D_MODEL), ATT_WIDTH ** -0.5),
        "rel_bias": nrm(ks[15], (REL_BUCKETS, ATT_HEADS), 0.5),
        "w_out": nrm(ks[16], (DEPTH, D_MODEL, D_MODEL), D_MODEL ** -0.5),
        "norm2_g": 1.0 + nrm(ks[17], (DEPTH, D_MODEL), 0.05),
        "peer_w_q": nrm(ks[18], (DEPTH, D_MODEL, PEER_HEADS * PEER_QDIM), D_MODEL ** -0.5),
        "peer_sub_k1": nrm(ks[19], (DEPTH, PEER_NKEYS, PEER_QDIM // 2), (PEER_QDIM // 2) ** -0.5),
        "peer_sub_k2": nrm(ks[20], (DEPTH, PEER_NKEYS, PEER_QDIM // 2), (PEER_QDIM // 2) ** -0.5),
        "peer_u": nrm(ks[21], (DEPTH, PEER_EXPERTS, D_MODEL), D_MODEL ** -0.5),
        "peer_v": nrm(ks[22], (DEPTH, PEER_EXPERTS, D_MODEL), PEER_HEADS ** -0.5),
        "final_g": 1.0 + nrm(ks[23], (D_MODEL,), 0.05),
    }


def reference(x, norm1_g, w_in, ssm_a_re, ssm_a_im, ssm_log_dt, ssm_b_re, ssm_b_im, ssm_c_re, ssm_c_im,
              ssm_d, ssm_w_glu, ssm_b_glu, w_up_ssm, w_up_att, rel_bias, w_out, norm2_g, peer_w_q,
              peer_sub_k1, peer_sub_k2, peer_u, peer_v, final_g):
    for l in range(DEPTH):
        h = rms_norm(x, norm1_g[l])
        z = h @ w_in[l]
        u_ssm, q, k, v, g_a, g_b = jnp.split(z, IN_SPLITS, axis=-1)
        y_a = s5_branch(u_ssm, ssm_a_re[l], ssm_a_im[l], ssm_log_dt[l], ssm_b_re[l], ssm_b_im[l],
                        ssm_c_re[l], ssm_c_im[l], ssm_d[l], ssm_w_glu[l], ssm_b_glu[l])
        y_b = moba_branch(q, k, v, rel_bias)
        merged = (jax.nn.sigmoid(g_a) * (y_a @ w_up_ssm[l])
                  + jax.nn.sigmoid(g_b) * (y_b @ w_up_att[l]))
        x = x + merged @ w_out[l]
        x = x + peer_ffn(rms_norm(x, norm2_g[l]), peer_w_q[l], peer_sub_k1[l], peer_sub_k2[l],
                         peer_u[l], peer_v[l])
    return rms_norm(x, final_g)
```

```python
import functools
import math

import jax
import jax.numpy as jnp
from jax import lax
from jax.experimental import pallas as pl
from jax.experimental.pallas import tpu as pltpu

F32 = jnp.float32
BF16 = jnp.bfloat16

SSM_GROUP = 16
SSM_GROUPS = 32
SSM_WIDTH = SSM_GROUP * SSM_GROUPS
SSM_STATE = 64
ATT_HEADS = 8
HEAD_DIM = 64
ATT_WIDTH = ATT_HEADS * HEAD_DIM
MOBA_BLOCK = 256
MOBA_TOPK = 3
REL_BUCKETS = 32
REL_MAX_DIST = 128
PEER_HEADS = 8
PEER_NKEYS = 128
PEER_QDIM = 256
PEER_TOPK = 16
RMS_EPS = 1e-6
NEG = -1e30

LANES = 128
SUBLANES = 8
VMEM_LIMIT_BYTES = 56 * 1024 * 1024

IN_PROJ_ROWS = 512
S5_CHUNK = 64
S5_SCAN_COLS = 1024
PEER_TOKENS = 512
PEER_ROUTE_TOKENS = 256
PEER_KEYS_PER_STEP = 8


def _gelu(x):
    c = math.sqrt(2.0 / math.pi)
    return 0.5 * x * (1.0 + jnp.tanh(c * (x + 0.044715 * (x * x * x))))


def _sigmoid(x):
    return 1.0 / (1.0 + jnp.exp(-x))


def _rms(x, g):
    ms = jnp.mean(x * x, axis=-1, keepdims=True)
    return (x * lax.rsqrt(ms + RMS_EPS)) * g


def _params(*sem):
    return pltpu.CompilerParams(dimension_semantics=sem, vmem_limit_bytes=VMEM_LIMIT_BYTES)


def _in_proj_kernel(x_ref, g_ref, w_ref, u_ref, q_ref, k_ref, v_ref, ga_ref, gb_ref):
    h = _rms(x_ref[0], g_ref[...]).astype(BF16)

    def proj(lo, hi):
        return jnp.dot(h, w_ref[:, lo:hi], preferred_element_type=F32)

    d = x_ref.shape[-1]
    o = 0
    u_ref[...] = proj(o, o + SSM_WIDTH); o += SSM_WIDTH
    q_ref[0] = proj(o, o + ATT_WIDTH).astype(BF16); o += ATT_WIDTH
    k_ref[0] = proj(o, o + ATT_WIDTH).astype(BF16); o += ATT_WIDTH
    v_ref[0] = proj(o, o + ATT_WIDTH).astype(BF16); o += ATT_WIDTH
    ga_ref[0] = _sigmoid(proj(o, o + d)).astype(BF16); o += d
    gb_ref[0] = _sigmoid(proj(o, o + d)).astype(BF16)


def _in_proj(x, g, w_bf16):
    b, l, d = x.shape
    rows = min(IN_PROJ_ROWS, l)
    cols = w_bf16.shape[1]
    tok = lambda width: pl.BlockSpec((1, rows, width), lambda i, j: (i, j, 0))
    return pl.pallas_call(
        _in_proj_kernel,
        grid=(b, l // rows),
        in_specs=[tok(d),
                  pl.BlockSpec((1, d), lambda i, j: (0, 0)),
                  pl.BlockSpec((d, cols), lambda i, j: (0, 0))],
        out_specs=[pl.BlockSpec((rows, SSM_WIDTH), lambda i, j: (j, i)),
                   tok(ATT_WIDTH), tok(ATT_WIDTH), tok(ATT_WIDTH), tok(d), tok(d)],
        out_shape=[jax.ShapeDtypeStruct((l, b * SSM_WIDTH), F32),
                   jax.ShapeDtypeStruct((b, l, ATT_WIDTH), BF16),
                   jax.ShapeDtypeStruct((b, l, ATT_WIDTH), BF16),
                   jax.ShapeDtypeStruct((b, l, ATT_WIDTH), BF16),
                   jax.ShapeDtypeStruct((b, l, d), BF16),
                   jax.ShapeDtypeStruct((b, l, d), BF16)],
        compiler_params=_params("parallel", "parallel"),
        name="in_proj",
    )(x, g.reshape(1, d), w_bf16)


def _s5_kernel(u_ref, bmat_ref, lam_ref, cmat_ref, d_ref, wglu_ref, bglu_ref, y_ref, s_ref, x_ref,
               *, batch):
    half_in = SSM_WIDTH // 2
    half_st = SSM_GROUPS * SSM_STATE // 2
    n_st = 2 * half_st
    steps = u_ref.shape[0] // batch

    @pl.when(pl.program_id(0) == 0)
    def _():
        x_ref[...] = jnp.zeros_like(x_ref)

    u = u_ref[...]
    ub = u.astype(BF16)
    for hf in range(2):
        bu = jnp.dot(ub[:, hf * half_in:(hf + 1) * half_in], bmat_ref[hf], preferred_element_type=F32)
        s_ref[:, hf * half_st:(hf + 1) * half_st] = bu[:, :half_st]
        s_ref[:, n_st + hf * half_st:n_st + (hf + 1) * half_st] = bu[:, half_st:]

    for c0 in range(0, n_st, S5_SCAN_COLS):
        re_cols = pl.ds(c0, S5_SCAN_COLS)
        im_cols = pl.ds(n_st + c0, S5_SCAN_COLS)
        lre = jnp.broadcast_to(lam_ref[0:1, c0:c0 + S5_SCAN_COLS], (batch, S5_SCAN_COLS))
        lim = jnp.broadcast_to(lam_ref[1:2, c0:c0 + S5_SCAN_COLS], (batch, S5_SCAN_COLS))

        def step(t, carry):
            xre, xim = carry
            rows = pl.ds(pl.multiple_of(t * batch, batch), batch)
            nre = lre * xre - lim * xim + s_ref[rows, re_cols]
            nim = lre * xim + lim * xre + s_ref[rows, im_cols]
            s_ref[rows, re_cols] = nre
            s_ref[rows, im_cols] = nim
            return nre, nim

        xre, xim = lax.fori_loop(0, steps, step, (x_ref[:, re_cols], x_ref[:, im_cols]), unroll=4)
        x_ref[:, re_cols] = xre
        x_ref[:, im_cols] = xim

    ys = []
    for hf in range(2):
        sre = s_ref[:, hf * half_st:(hf + 1) * half_st].astype(BF16)
        sim = s_ref[:, n_st + hf * half_st:n_st + (hf + 1) * half_st].astype(BF16)
        ys.append(jnp.dot(sre, cmat_ref[hf, :half_st], preferred_element_type=F32)
                  + jnp.dot(sim, cmat_ref[hf, half_st:], preferred_element_type=F32))
    y = jnp.concatenate(ys, axis=1) + d_ref[...] * u
    y = _gelu(y)
    gate = jnp.dot(y.astype(BF16), wglu_ref[...], preferred_element_type=F32) + bglu_ref[...]
    y_ref[...] = (y * _sigmoid(gate)).astype(BF16)


def _s5_matrices(a_re, a_im, log_dt, b_re, b_im, c_re, c_im):
    g, p, h = SSM_GROUPS, SSM_STATE, SSM_GROUP
    lam = lax.complex(a_re.astype(F32), a_im.astype(F32))
    dt = jnp.exp(log_dt.astype(F32))[:, None]
    lam_bar = jnp.exp(lam * dt)
    b_bar = ((lam_bar - 1.0) / lam)[..., None] * lax.complex(b_re.astype(F32), b_im.astype(F32))
    hg = g // 2
    eye = jnp.eye(hg, dtype=F32)

    def in_block(m):
        return jnp.einsum("gph,gk->ghkp", m, eye).reshape(hg * h, hg * p)

    def out_block(m):
        return jnp.einsum("ghp,gk->gpkh", m, eye).reshape(hg * p, hg * h)

    bmat = jnp.stack([jnp.concatenate([in_block(jnp.real(b_bar)[s]), in_block(jnp.imag(b_bar)[s])], axis=1)
                      for s in (slice(0, hg), slice(hg, g))])
    cmat = jnp.stack([jnp.concatenate([out_block(c_re.astype(F32)[s]), out_block(-c_im.astype(F32)[s])], axis=0)
                      for s in (slice(0, hg), slice(hg, g))])
    lam_rows = jnp.stack([jnp.real(lam_bar).reshape(-1), jnp.imag(lam_bar).reshape(-1)])
    return bmat.astype(BF16), lam_rows, cmat.astype(BF16)


def _s5(u_tm, batch, bmat, lam_rows, cmat, d_skip, w_glu_bf16, b_glu):
    rows_total, w = u_tm.shape
    steps = min(S5_CHUNK, rows_total // batch)
    rows = steps * batch
    n_st2 = 2 * SSM_GROUPS * SSM_STATE
    full = lambda a: pl.BlockSpec(a.shape, lambda i: (0,) * a.ndim)
    d_row = d_skip.reshape(1, w).astype(F32)
    b_row = b_glu.reshape(1, w).astype(F32)
    return pl.pallas_call(
        functools.partial(_s5_kernel, batch=batch),
        grid=(rows_total // rows,),
        in_specs=[pl.BlockSpec((rows, w), lambda i: (i, 0)),
                  full(bmat), full(lam_rows), full(cmat), full(d_row), full(w_glu_bf16), full(b_row)],
        out_specs=pl.BlockSpec((rows, w), lambda i: (i, 0)),
        out_shape=jax.ShapeDtypeStruct((rows_total, w), BF16),
        scratch_shapes=[pltpu.VMEM((rows, n_st2), F32), pltpu.VMEM((batch, n_st2), F32)],
        compiler_params=_params("arbitrary"),
        name="s5",
    )(u_tm, bmat, lam_rows, cmat, d_row, w_glu_bf16, b_row)


def _t5_bucket(rel):
    n = jnp.maximum(rel, 0)
    max_exact = REL_BUCKETS // 2
    nf = jnp.maximum(n, 1).astype(F32)
    large = max_exact + (jnp.log(nf / max_exact) / math.log(REL_MAX_DIST / max_exact)
                         * (REL_BUCKETS - max_exact)).astype(jnp.int32)
    large = jnp.clip(large, 0, REL_BUCKETS - 1)
    return jnp.where(n < max_exact, n, large)


def _moba_bias_tiles(rel_bias):
    table = rel_bias.astype(F32).T
    offs = jnp.arange(MOBA_BLOCK)
    rel_own = offs[:, None] - offs[None, :]
    own = jnp.where(rel_own >= 0, table[:, _t5_bucket(rel_own)], NEG)
    prev = table[:, _t5_bucket(rel_own + MOBA_BLOCK)]
    far = table[:, _t5_bucket(jnp.full((), 2 * MOBA_BLOCK - (MOBA_BLOCK - 1), jnp.int32))]
    return own, prev, far


def _moba_kernel(q_ref, k_ref, v_ref, own_ref, prev_ref, far_ref, o_ref, kaug_ref, *, n_blocks):
    blk = MOBA_BLOCK
    seq = n_blocks * blk
    scale = HEAD_DIM ** -0.5
    k = k_ref[0]
    v = v_ref[0]
    lane = lax.broadcasted_iota(jnp.int32, (1, LANES), 1)

    row_blk = lax.broadcasted_iota(jnp.int32, (SUBLANES, seq), 1) // blk
    avg = jnp.where(row_blk == lax.broadcasted_iota(jnp.int32, (SUBLANES, seq), 0), 1.0 / blk, 0.0).astype(BF16)
    kmean = jnp.dot(avg, k, preferred_element_type=F32)

    key_blk = lax.broadcasted_iota(jnp.int32, (seq, LANES), 0) // blk
    key_lane = lax.broadcasted_iota(jnp.int32, (seq, LANES), 1)

    outs = [[None, None] for _ in range(n_blocks)]
    for hh in range(2):
        head_lo = hh * HEAD_DIM
        base = HEAD_DIM - head_lo
        in_head = (lane >= head_lo) & (lane < head_lo + HEAD_DIM)
        slot = lane - base
        in_slot = (slot >= 0) & (slot < 2 * SUBLANES)
        slot_blk = jnp.where(in_slot, slot % SUBLANES, -1)

        kslot = key_lane - base
        k_onehot = ((kslot >= 0) & (kslot < 2 * SUBLANES) & (kslot % SUBLANES == key_blk))
        in_head_k = (key_lane >= head_lo) & (key_lane < head_lo + HEAD_DIM)
        kaug_ref[...] = jnp.where(in_head_k, k, jnp.where(k_onehot, 1.0, 0.0).astype(BF16))

        pad_a = [jnp.zeros((base, LANES), F32)] if base else []
        pad_b = [jnp.zeros((LANES - base - 2 * SUBLANES, LANES), F32)]
        kmm = jnp.concatenate(pad_a + [kmean, kmean] + pad_b, axis=0)
        kmm_hi = kmm.astype(BF16)
        kmm_lo = (kmm - kmm_hi.astype(F32)).astype(BF16)
        far = far_ref[0, hh:hh + 1, :]

        for i in range(n_blocks):
            q = q_ref[0, i * blk:(i + 1) * blk, :]
            qs = jnp.where(in_head, q * scale, 0.0).astype(BF16)
            nt = (((1,), (1,)), ((), ()))
            gate = (lax.dot_general(qs, kmm_hi, nt, preferred_element_type=F32)
                    + lax.dot_general(qs, kmm_lo, nt, preferred_element_type=F32))
            rank = jnp.zeros((blk, LANES), F32)
            for j in range(i):
                col = gate[:, base + j:base + j + 1]
                beats = (col > gate) | ((col == gate) & (j < slot_blk))
                rank = rank + jnp.where(beats, 1.0, 0.0)
            chosen = rank < float(MOBA_TOPK)
            is_past = (slot_blk >= 0) & (slot_blk < i)
            is_far = (slot_blk >= 0) & (slot_blk < i - 1)
            add = jnp.where(is_past, jnp.where(chosen, jnp.where(is_far, far, 0.0), NEG), 0.0)
            add_hi = add.astype(BF16)
            add_lo = jnp.where(add > 0.5 * NEG, add - add_hi.astype(F32), 0.0).astype(BF16)
            mask_vals = jnp.where(slot < SUBLANES, add_hi, add_lo)
            q_aug = jnp.where(in_head, qs, jnp.where(in_slot, mask_vals, 0.0).astype(BF16))

            nk = (i + 1) * blk
            s = lax.dot_general(q_aug, kaug_ref[0:nk, :], nt, preferred_element_type=F32)
            pieces = []
            if i >= 2:
                pieces.append(s[:, :(i - 1) * blk])
            if i >= 1:
                pieces.append(s[:, (i - 1) * blk:i * blk] + prev_ref[hh])
            pieces.append(s[:, i * blk:] + own_ref[hh])
            m = functools.reduce(jnp.maximum, [jnp.max(p, axis=-1, keepdims=True) for p in pieces])
            ps = [jnp.exp(p - m) for p in pieces]
            denom = functools.reduce(lambda a, b: a + b, [jnp.sum(p, axis=-1, keepdims=True) for p in ps])
            p_all = jnp.concatenate(ps, axis=-1).astype(BF16) if len(ps) > 1 else ps[0].astype(BF16)
            o = jnp.dot(p_all, v[0:nk, :], preferred_element_type=F32) / denom
            outs[i][hh] = o

    for i in range(n_blocks):
        o_ref[0, i * blk:(i + 1) * blk, :] = jnp.where(lane < HEAD_DIM, outs[i][0], outs[i][1]).astype(BF16)


def _moba(q, k, v, own, prev, far):
    b, l, w = q.shape
    assert l % MOBA_BLOCK == 0, "sequence must be a whole number of MoBA blocks"
    n_blocks = l // MOBA_BLOCK
    assert n_blocks <= SUBLANES
    pairs = w // LANES
    far_rows = jnp.broadcast_to(far.reshape(pairs, 2, 1), (pairs, 2, LANES))
    tok = pl.BlockSpec((1, l, LANES), lambda i, j: (i, 0, j))
    tile = pl.BlockSpec((2, MOBA_BLOCK, MOBA_BLOCK), lambda i, j: (j, 0, 0))
    return pl.pallas_call(
        functools.partial(_moba_kernel, n_blocks=n_blocks),
        grid=(b, pairs),
        in_specs=[tok, tok, tok, tile, tile, pl.BlockSpec((1, 2, LANES), lambda i, j: (j, 0, 0))],
        out_specs=tok,
        out_shape=jax.ShapeDtypeStruct((b, l, w), BF16),
        scratch_shapes=[pltpu.VMEM((l, LANES), BF16)],
        compiler_params=_params("parallel", "parallel"),
        name="moba",
    )(q, k, v, own, prev, far_rows)


def _merge_kernel(x_ref, ya_ref, yb_ref, ga_ref, gb_ref, wa_ref, wb_ref, wo_ref, g2_ref, x2_ref, hn_t_ref):
    up_a = jnp.dot(ya_ref[...], wa_ref[...], preferred_element_type=F32)
    up_b = jnp.dot(yb_ref[0], wb_ref[...], preferred_element_type=F32)
    merged = ga_ref[0].astype(F32) * up_a + gb_ref[0].astype(F32) * up_b
    x2 = x_ref[0] + jnp.dot(merged.astype(BF16), wo_ref[...], preferred_element_type=F32)
    x2_ref[0] = x2
    hn_t_ref[...] = _rms(x2, g2_ref[...]).T.astype(BF16)


def _merge(x, ya_tm, yb, ga, gb, wa, wb, wo, g2):
    b, l, d = x.shape
    rows = min(IN_PROJ_ROWS, l)
    nj = l // rows
    tok = lambda width: pl.BlockSpec((1, rows, width), lambda i, j: (i, j, 0))
    full = lambda a: pl.BlockSpec(a.shape, lambda i, j: (0,) * a.ndim)
    g2r = g2.reshape(1, d)
    return pl.pallas_call(
        _merge_kernel,
        grid=(b, nj),
        in_specs=[tok(d),
                  pl.BlockSpec((rows, SSM_WIDTH), lambda i, j: (j, i)),
                  tok(ATT_WIDTH), tok(d), tok(d), full(wa), full(wb), full(wo), full(g2r)],
        out_specs=[tok(d), pl.BlockSpec((d, rows), lambda i, j: (0, i * nj + j))],
        out_shape=[jax.ShapeDtypeStruct((b, l, d), F32), jax.ShapeDtypeStruct((d, b * l), BF16)],
        compiler_params=_params("parallel", "parallel"),
        name="merge",
    )(x, ya_tm, yb, ga, gb, wa, wb, wo, g2r)


def _top_rows(x, n):
    rows = []
    cur = x
    for r in range(n):
        mx = jnp.max(cur, axis=0, keepdims=True)
        rows.append(mx)
        if r + 1 < n:
            cur = jnp.where(cur == mx, -jnp.inf, cur)
    return rows


def _route_kernel(hn_t_ref, wq_t_ref, k1_ref, k2_ref, s1_ref, s2_ref, e1_ref, e2_ref, tau_ref):
    half = PEER_QDIM // 2
    hn_t = hn_t_ref[...]
    cols = hn_t.shape[1]
    row8 = lax.broadcasted_iota(jnp.int32, (SUBLANES, cols), 0)
    taus = []
    for h in range(PEER_HEADS):
        q_t = jnp.dot(wq_t_ref[h * PEER_QDIM:(h + 1) * PEER_QDIM, :], hn_t, preferred_element_type=F32)
        s1 = jnp.dot(k1_ref[...], q_t[:half].astype(BF16), preferred_element_type=F32)
        s2 = jnp.dot(k2_ref[...], q_t[half:].astype(BF16), preferred_element_type=F32)
        v1 = _top_rows(s1, PEER_TOPK)
        v2 = _top_rows(s2, PEER_TOPK)
        v2_all = jnp.concatenate(v2, axis=0)
        v2_top = v2_all[:SUBLANES]
        cands = [v1[0] + v2_all, v1[1] + v2_top]
        for i in range(2, SUBLANES):
            cands.append(jnp.where(row8 < PEER_TOPK // (i + 1), v1[i] + v2_top, -jnp.inf))
        cands.append(jnp.concatenate(v1[SUBLANES:], axis=0) + v2[0])
        best = _top_rows(jnp.concatenate(cands, axis=0), PEER_TOPK)
        top = best[0]
        z = functools.reduce(lambda a, b: a + b, [jnp.exp(c - top) for c in best])
        s1_ref[h] = s1
        s2_ref[h] = s2
        e1_ref[h] = jnp.exp(s1 - v1[0]) / z
        e2_ref[h] = jnp.exp(s2 - v2[0])
        taus.append(best[-1])
    tau_ref[...] = jnp.concatenate(taus, axis=0)


def _route(hn_t, wq_t, k1, k2):
    d, t = hn_t.shape
    cols = min(PEER_ROUTE_TOKENS, t)
    full = lambda a: pl.BlockSpec(a.shape, lambda i: (0,) * a.ndim)
    per_head = pl.BlockSpec((PEER_HEADS, PEER_NKEYS, cols), lambda i: (0, 0, i))
    shape = jax.ShapeDtypeStruct((PEER_HEADS, PEER_NKEYS, t), F32)
    return pl.pallas_call(
        _route_kernel,
        grid=(t // cols,),
        in_specs=[pl.BlockSpec((d, cols), lambda i: (0, i)), full(wq_t), full(k1), full(k2)],
        out_specs=[per_head, per_head, per_head, per_head, pl.BlockSpec((PEER_HEADS, cols), lambda i: (0, i))],
        out_shape=[shape, shape, shape, shape, jax.ShapeDtypeStruct((PEER_HEADS, t), F32)],
        compiler_params=_params("parallel"),
        name="peer_route",
    )(hn_t, wq_t, k1, k2)


def _peer_kernel(hn_t_ref, u_ref, vt_ref, s1_ref, s2_ref, e1_ref, e2_ref, tau_ref, x2_ref, gf_ref,
                 o_ref, act_ref, w_ref, acc_ref):
    c = pl.program_id(1)
    n_chunks = pl.num_programs(1)
    cols = hn_t_ref.shape[1]
    keys = PEER_NKEYS

    @pl.when(c == 0)
    def _():
        acc_ref[...] = jnp.zeros_like(acc_ref)

    act_ref[...] = jnp.dot(u_ref[...], hn_t_ref[...], preferred_element_type=F32)

    first_keys = pl.ds(pl.multiple_of(c * PEER_KEYS_PER_STEP, PEER_KEYS_PER_STEP), PEER_KEYS_PER_STEP)

    def build(lc, carry):
        lanes = pl.ds(pl.multiple_of(lc * LANES, LANES), LANES)
        for a_loc in range(PEER_KEYS_PER_STEP):
            rows = pl.ds(a_loc * keys, keys)
            gate = jnp.zeros((keys, LANES), F32)
            for h in range(PEER_HEADS):
                s1a = s1_ref[h, first_keys, lanes][a_loc:a_loc + 1, :]
                e1a = e1_ref[h, first_keys, lanes][a_loc:a_loc + 1, :]
                tau = tau_ref[h:h + 1, lanes]
                sel = (s1a + s2_ref[h, :, lanes]) >= tau
                gate = gate + jnp.where(sel, e1a * e2_ref[h, :, lanes], 0.0)
            w_ref[rows, lanes] = (gate * _gelu(act_ref[rows, lanes])).astype(BF16)
        return carry

    lax.fori_loop(0, cols // LANES, build, 0)
    acc_ref[...] += jnp.dot(vt_ref[...], w_ref[...], preferred_element_type=F32)

    @pl.when(c == n_chunks - 1)
    def _():
        x3 = x2_ref[...] + acc_ref[...].T
        o_ref[...] = _rms(x3, gf_ref[...])


def _peer(hn_t, u_bf16, vt_bf16, s1, s2, e1, e2, tau, x2_flat, gf):
    d, t = hn_t.shape
    n_exp = u_bf16.shape[0]
    cols = min(PEER_TOKENS, t)
    chunk = PEER_KEYS_PER_STEP * PEER_NKEYS
    per_head = pl.BlockSpec((PEER_HEADS, PEER_NKEYS, cols), lambda i, c: (0, 0, i))
    gfr = gf.reshape(1, d)
    return pl.pallas_call(
        _peer_kernel,
        grid=(t // cols, n_exp // chunk),
        in_specs=[pl.BlockSpec((d, cols), lambda i, c: (0, i)),
                  pl.BlockSpec((chunk, d), lambda i, c: (c, 0)),
                  pl.BlockSpec((d, chunk), lambda i, c: (0, c)),
                  per_head, per_head, per_head, per_head,
                  pl.BlockSpec((PEER_HEADS, cols), lambda i, c: (0, i)),
                  pl.BlockSpec((cols, d), lambda i, c: (i, 0)),
                  pl.BlockSpec((1, d), lambda i, c: (0, 0))],
        out_specs=pl.BlockSpec((cols, d), lambda i, c: (i, 0)),
        out_shape=jax.ShapeDtypeStruct((t, d), F32),
        scratch_shapes=[pltpu.VMEM((chunk, cols), F32), pltpu.VMEM((chunk, cols), BF16),
                        pltpu.VMEM((d, cols), F32)],
        compiler_params=_params("parallel", "arbitrary"),
        name="peer",
    )(hn_t, u_bf16, vt_bf16, s1, s2, e1, e2, tau, x2_flat, gfr)


def _layer(x, norm1_g, w_in, a_re, a_im, log_dt, b_re, b_im, c_re, c_im, d_skip, w_glu, b_glu,
           w_up_ssm, w_up_att, bias_tiles, w_out, norm2_g, w_q, k1, k2, u_emb, v_emb, out_g):
    b, l, d = x.shape
    u_tm, q, k, v, ga, gb = _in_proj(x, norm1_g, w_in.astype(BF16))
    bmat, lam_rows, cmat = _s5_matrices(a_re, a_im, log_dt, b_re, b_im, c_re, c_im)
    ya = _s5(u_tm.reshape(l * b, SSM_WIDTH), b, bmat, lam_rows, cmat, d_skip.reshape(-1),
             w_glu.astype(BF16), b_glu)
    yb = _moba(q, k, v, *bias_tiles)
    x2, hn_t = _merge(x, ya.reshape(l, b * SSM_WIDTH), yb, ga, gb, w_up_ssm.astype(BF16),
                      w_up_att.astype(BF16), w_out.astype(BF16), norm2_g)
    s1, s2, e1, e2, tau = _route(hn_t, w_q.T.astype(BF16), k1.astype(BF16), k2.astype(BF16))
    out = _peer(hn_t, u_emb.astype(BF16), v_emb.T.astype(BF16), s1, s2, e1, e2, tau,
                x2.reshape(b * l, d), out_g)
    return out.reshape(b, l, d)


def kernel(x, norm1_g, w_in, ssm_a_re, ssm_a_im, ssm_log_dt, ssm_b_re, ssm_b_im, ssm_c_re, ssm_c_im, ssm_d, ssm_w_glu, ssm_b_glu, w_up_ssm, w_up_att, rel_bias, w_out, norm2_g, peer_w_q, peer_sub_k1, peer_sub_k2, peer_u, peer_v, final_g):
    assert w_in.shape[0] == 1, "only DEPTH == 1 is supported"
    bias_tiles = _moba_bias_tiles(rel_bias)
    l = 0
    return _layer(x, norm1_g[l], w_in[l], ssm_a_re[l], ssm_a_im[l], ssm_log_dt[l], ssm_b_re[l], ssm_b_im[l],
                  ssm_c_re[l], ssm_c_im[l], ssm_d[l], ssm_w_glu[l], ssm_b_glu[l], w_up_ssm[l], w_up_att[l],
                  bias_tiles, w_out[l], norm2_g[l], peer_w_q[l], peer_sub_k1[l], peer_sub_k2[l],
                  peer_u[l], peer_v[l], final_g)
```

```python
import functools
import math

import jax
import jax.numpy as jnp
from jax import lax
from jax.experimental import pallas as pl
from jax.experimental.pallas import tpu as pltpu

F32 = jnp.float32
BF16 = jnp.bfloat16

SSM_GROUP = 16
SSM_GROUPS = 32
SSM_WIDTH = SSM_GROUP * SSM_GROUPS
SSM_STATE = 64
ATT_HEADS = 8
HEAD_DIM = 64
ATT_WIDTH = ATT_HEADS * HEAD_DIM
MOBA_BLOCK = 256
MOBA_TOPK = 3
REL_BUCKETS = 32
REL_MAX_DIST = 128
PEER_HEADS = 8
PEER_NKEYS = 128
PEER_QDIM = 256
PEER_TOPK = 16
RMS_EPS = 1e-6
NEG = -1e30

LANES = 128
SUBLANES = 8
BF16_LANES = 256
VMEM_LIMIT_BYTES = 56 * 1024 * 1024

IN_PROJ_ROWS = 512
S5_CHUNK = 64
S5_SCAN_COLS = 1024
PEER_TOKENS = 512
PEER_ROUTE_TOKENS = 256
PEER_KEYS_PER_STEP = 8
MXU_ROWS = 256


def _gelu(x):
    c = math.sqrt(2.0 / math.pi)
    minus_2u = x * ((-2.0 * c * 0.044715) * (x * x) + (-2.0 * c))
    return x / (1.0 + jnp.exp(minus_2u))


def _sigmoid(x):
    return 1.0 / (1.0 + jnp.exp(-x))


def _rms(x, g):
    ms = jnp.mean(x * x, axis=-1, keepdims=True)
    return (x * lax.rsqrt(ms + RMS_EPS)) * g


def _params(*sem):
    return pltpu.CompilerParams(dimension_semantics=sem, vmem_limit_bytes=VMEM_LIMIT_BYTES)


def _in_proj_kernel(x_ref, g_ref, w_ref, u_ref, q_ref, k_ref, v_ref, ga_ref, gb_ref):
    h = _rms(x_ref[0], g_ref[...]).astype(BF16)

    def proj(lo, hi):
        return jnp.dot(h, w_ref[:, lo:hi], preferred_element_type=F32)

    d = x_ref.shape[-1]
    o = 0
    u_ref[...] = proj(o, o + SSM_WIDTH); o += SSM_WIDTH
    q_ref[0] = proj(o, o + ATT_WIDTH).astype(BF16); o += ATT_WIDTH
    k_ref[0] = proj(o, o + ATT_WIDTH).astype(BF16); o += ATT_WIDTH
    v_ref[0] = proj(o, o + ATT_WIDTH).astype(BF16); o += ATT_WIDTH
    ga_ref[0] = _sigmoid(proj(o, o + d)).astype(BF16); o += d
    gb_ref[0] = _sigmoid(proj(o, o + d)).astype(BF16)


def _in_proj(x, g, w_bf16):
    b, l, d = x.shape
    rows = min(IN_PROJ_ROWS, l)
    cols = w_bf16.shape[1]
    tok = lambda width: pl.BlockSpec((1, rows, width), lambda i, j: (i, j, 0))
    return pl.pallas_call(
        _in_proj_kernel,
        grid=(b, l // rows),
        in_specs=[tok(d),
                  pl.BlockSpec((1, d), lambda i, j: (0, 0)),
                  pl.BlockSpec((d, cols), lambda i, j: (0, 0))],
        out_specs=[pl.BlockSpec((rows, SSM_WIDTH), lambda i, j: (j, i)),
                   tok(ATT_WIDTH), tok(ATT_WIDTH), tok(ATT_WIDTH), tok(d), tok(d)],
        out_shape=[jax.ShapeDtypeStruct((l, b * SSM_WIDTH), F32),
                   jax.ShapeDtypeStruct((b, l, ATT_WIDTH), BF16),
                   jax.ShapeDtypeStruct((b, l, ATT_WIDTH), BF16),
                   jax.ShapeDtypeStruct((b, l, ATT_WIDTH), BF16),
                   jax.ShapeDtypeStruct((b, l, d), BF16),
                   jax.ShapeDtypeStruct((b, l, d), BF16)],
        compiler_params=_params("parallel", "parallel"),
        name="in_proj",
    )(x, g.reshape(1, d), w_bf16)


def _s5_kernel(u_ref, bmat_ref, lam_ref, cmat_ref, d_ref, wglu_ref, bglu_ref, y_ref, s_ref, x_ref,
               *, batch):
    half_in = SSM_WIDTH // 2
    half_st = SSM_GROUPS * SSM_STATE // 2
    n_st = 2 * half_st
    steps = u_ref.shape[0] // batch

    @pl.when(pl.program_id(0) == 0)
    def _():
        x_ref[...] = jnp.zeros_like(x_ref)

    u = u_ref[...]
    ub = u.astype(BF16)
    for hf in range(2):
        bu = jnp.dot(ub[:, hf * half_in:(hf + 1) * half_in], bmat_ref[hf], preferred_element_type=F32)
        s_ref[:, hf * half_st:(hf + 1) * half_st] = bu[:, :half_st]
        s_ref[:, n_st + hf * half_st:n_st + (hf + 1) * half_st] = bu[:, half_st:]

    for c0 in range(0, n_st, S5_SCAN_COLS):
        re_cols = pl.ds(c0, S5_SCAN_COLS)
        im_cols = pl.ds(n_st + c0, S5_SCAN_COLS)
        lre = jnp.broadcast_to(lam_ref[0:1, c0:c0 + S5_SCAN_COLS], (batch, S5_SCAN_COLS))
        lim = jnp.broadcast_to(lam_ref[1:2, c0:c0 + S5_SCAN_COLS], (batch, S5_SCAN_COLS))

        def step(t, carry):
            xre, xim = carry
            rows = pl.ds(pl.multiple_of(t * batch, batch), batch)
            nre = lre * xre - lim * xim + s_ref[rows, re_cols]
            nim = lre * xim + lim * xre + s_ref[rows, im_cols]
            s_ref[rows, re_cols] = nre
            s_ref[rows, im_cols] = nim
            return nre, nim

        xre, xim = lax.fori_loop(0, steps, step, (x_ref[:, re_cols], x_ref[:, im_cols]), unroll=4)
        x_ref[:, re_cols] = xre
        x_ref[:, im_cols] = xim

    ys = []
    for hf in range(2):
        sre = s_ref[:, hf * half_st:(hf + 1) * half_st].astype(BF16)
        sim = s_ref[:, n_st + hf * half_st:n_st + (hf + 1) * half_st].astype(BF16)
        ys.append(jnp.dot(sre, cmat_ref[hf, :half_st], preferred_element_type=F32)
                  + jnp.dot(sim, cmat_ref[hf, half_st:], preferred_element_type=F32))
    y = jnp.concatenate(ys, axis=1) + d_ref[...] * u
    y = _gelu(y)
    gate = jnp.dot(y.astype(BF16), wglu_ref[...], preferred_element_type=F32) + bglu_ref[...]
    y_ref[...] = (y * _sigmoid(gate)).astype(BF16)


def _s5_matrices(a_re, a_im, log_dt, b_re, b_im, c_re, c_im):
    g, p, h = SSM_GROUPS, SSM_STATE, SSM_GROUP
    lam = lax.complex(a_re.astype(F32), a_im.astype(F32))
    dt = jnp.exp(log_dt.astype(F32))[:, None]
    lam_bar = jnp.exp(lam * dt)
    b_bar = ((lam_bar - 1.0) / lam)[..., None] * lax.complex(b_re.astype(F32), b_im.astype(F32))
    hg = g // 2
    eye = jnp.eye(hg, dtype=F32)

    def in_block(m):
        return jnp.einsum("gph,gk->ghkp", m, eye).reshape(hg * h, hg * p)

    def out_block(m):
        return jnp.einsum("ghp,gk->gpkh", m, eye).reshape(hg * p, hg * h)

    bmat = jnp.stack([jnp.concatenate([in_block(jnp.real(b_bar)[s]), in_block(jnp.imag(b_bar)[s])], axis=1)
                      for s in (slice(0, hg), slice(hg, g))])
    cmat = jnp.stack([jnp.concatenate([out_block(c_re.astype(F32)[s]), out_block(-c_im.astype(F32)[s])], axis=0)
                      for s in (slice(0, hg), slice(hg, g))])
    lam_rows = jnp.stack([jnp.real(lam_bar).reshape(-1), jnp.imag(lam_bar).reshape(-1)])
    return bmat.astype(BF16), lam_rows, cmat.astype(BF16)


def _s5(u_tm, batch, bmat, lam_rows, cmat, d_skip, w_glu_bf16, b_glu):
    rows_total, w = u_tm.shape
    steps = min(S5_CHUNK, rows_total // batch)
    rows = steps * batch
    n_st2 = 2 * SSM_GROUPS * SSM_STATE
    full = lambda a: pl.BlockSpec(a.shape, lambda i: (0,) * a.ndim)
    d_row = d_skip.reshape(1, w).astype(F32)
    b_row = b_glu.reshape(1, w).astype(F32)
    return pl.pallas_call(
        functools.partial(_s5_kernel, batch=batch),
        grid=(rows_total // rows,),
        in_specs=[pl.BlockSpec((rows, w), lambda i: (i, 0)),
                  full(bmat), full(lam_rows), full(cmat), full(d_row), full(w_glu_bf16), full(b_row)],
        out_specs=pl.BlockSpec((rows, w), lambda i: (i, 0)),
        out_shape=jax.ShapeDtypeStruct((rows_total, w), BF16),
        scratch_shapes=[pltpu.VMEM((rows, n_st2), F32), pltpu.VMEM((batch, n_st2), F32)],
        compiler_params=_params("arbitrary"),
        name="s5",
    )(u_tm, bmat, lam_rows, cmat, d_row, w_glu_bf16, b_row)


def _t5_bucket(rel):
    n = jnp.maximum(rel, 0)
    max_exact = REL_BUCKETS // 2
    nf = jnp.maximum(n, 1).astype(F32)
    large = max_exact + (jnp.log(nf / max_exact) / math.log(REL_MAX_DIST / max_exact)
                         * (REL_BUCKETS - max_exact)).astype(jnp.int32)
    large = jnp.clip(large, 0, REL_BUCKETS - 1)
    return jnp.where(n < max_exact, n, large)


def _moba_bias_tiles(rel_bias):
    table = rel_bias.astype(F32).T
    offs = jnp.arange(MOBA_BLOCK)
    rel_own = offs[:, None] - offs[None, :]

    def lookup(rel):
        onehot = (_t5_bucket(rel)[..., None] == jnp.arange(REL_BUCKETS)).astype(F32)
        return jnp.einsum("hb,qkb->hqk", table, onehot, precision=lax.Precision.HIGHEST)

    own = jnp.where(rel_own >= 0, lookup(rel_own), NEG)
    prev = lookup(rel_own + MOBA_BLOCK)
    assert MOBA_BLOCK + 1 >= REL_MAX_DIST
    far = table[:, REL_BUCKETS - 1]
    return own, prev, far


def _moba_kernel(q_ref, k_ref, v_ref, own_ref, prev_ref, far_ref, o_ref, kaug_ref, *, n_blocks):
    blk = MOBA_BLOCK
    seq = n_blocks * blk
    scale = HEAD_DIM ** -0.5
    k = k_ref[0]
    v = v_ref[0]
    lane = lax.broadcasted_iota(jnp.int32, (1, LANES), 1)

    row_blk = lax.broadcasted_iota(jnp.int32, (SUBLANES, seq), 1) // blk
    avg = jnp.where(row_blk == lax.broadcasted_iota(jnp.int32, (SUBLANES, seq), 0), 1.0 / blk, 0.0).astype(BF16)
    kmean = jnp.dot(avg, k, preferred_element_type=F32)

    key_blk = lax.broadcasted_iota(jnp.int32, (seq, LANES), 0) // blk
    key_lane = lax.broadcasted_iota(jnp.int32, (seq, LANES), 1)

    outs = [[None, None] for _ in range(n_blocks)]
    for hh in range(2):
        head_lo = hh * HEAD_DIM
        base = HEAD_DIM - head_lo
        in_head = (lane >= head_lo) & (lane < head_lo + HEAD_DIM)
        slot = lane - base
        in_slot = (slot >= 0) & (slot < 2 * SUBLANES)
        slot_blk = jnp.where(in_slot, slot % SUBLANES, -1)

        kslot = key_lane - base
        k_onehot = ((kslot >= 0) & (kslot < 2 * SUBLANES) & (kslot % SUBLANES == key_blk))
        in_head_k = (key_lane >= head_lo) & (key_lane < head_lo + HEAD_DIM)
        kaug_ref[...] = jnp.where(in_head_k, k, jnp.where(k_onehot, 1.0, 0.0).astype(BF16))

        pad_a = [jnp.zeros((base, LANES), F32)] if base else []
        pad_b = [jnp.zeros((LANES - base - 2 * SUBLANES, LANES), F32)]
        kmm = jnp.concatenate(pad_a + [kmean, kmean] + pad_b, axis=0)
        kmm_hi = kmm.astype(BF16)
        kmm_lo = (kmm - kmm_hi.astype(F32)).astype(BF16)
        far = far_ref[0, hh:hh + 1, :]

        for i in range(n_blocks):
            q = q_ref[0, i * blk:(i + 1) * blk, :]
            qs = jnp.where(in_head, q * scale, 0.0).astype(BF16)
            nt = (((1,), (1,)), ((), ()))
            gate = (lax.dot_general(qs, kmm_hi, nt, preferred_element_type=F32)
                    + lax.dot_general(qs, kmm_lo, nt, preferred_element_type=F32))
            rank = jnp.zeros((blk, LANES), F32)
            for j in range(i):
                col = gate[:, base + j:base + j + 1]
                beats = (col > gate) | ((col == gate) & (j < slot_blk))
                rank = rank + jnp.where(beats, 1.0, 0.0)
            chosen = rank < float(MOBA_TOPK)
            is_past = (slot_blk >= 0) & (slot_blk < i)
            is_far = (slot_blk >= 0) & (slot_blk < i - 1)
            add = jnp.where(is_past, jnp.where(chosen, jnp.where(is_far, far, 0.0), NEG), 0.0)
            add_hi = add.astype(BF16)
            add_lo = jnp.where(add > 0.5 * NEG, add - add_hi.astype(F32), 0.0).astype(BF16)
            mask_vals = jnp.where(slot < SUBLANES, add_hi, add_lo)
            q_aug = jnp.where(in_head, qs, jnp.where(in_slot, mask_vals, 0.0).astype(BF16))

            nk = (i + 1) * blk
            s = lax.dot_general(q_aug, kaug_ref[0:nk, :], nt, preferred_element_type=F32)
            pieces = []
            if i >= 2:
                pieces.append(s[:, :(i - 1) * blk])
            if i >= 1:
                pieces.append(s[:, (i - 1) * blk:i * blk] + prev_ref[hh])
            pieces.append(s[:, i * blk:] + own_ref[hh])
            m = functools.reduce(jnp.maximum, [jnp.max(p, axis=-1, keepdims=True) for p in pieces])
            ps = [jnp.exp(p - m) for p in pieces]
            denom = functools.reduce(lambda a, b: a + b, [jnp.sum(p, axis=-1, keepdims=True) for p in ps])
            p_all = jnp.concatenate(ps, axis=-1).astype(BF16) if len(ps) > 1 else ps[0].astype(BF16)
            o = jnp.dot(p_all, v[0:nk, :], preferred_element_type=F32) / denom
            outs[i][hh] = o

    for i in range(n_blocks):
        o_ref[0, i * blk:(i + 1) * blk, :] = jnp.where(lane < HEAD_DIM, outs[i][0], outs[i][1]).astype(BF16)


def _moba(q, k, v, own, prev, far):
    b, l, w = q.shape
    assert l % MOBA_BLOCK == 0, "sequence must be a whole number of MoBA blocks"
    n_blocks = l // MOBA_BLOCK
    assert n_blocks <= SUBLANES
    pairs = w // LANES
    far_rows = jnp.broadcast_to(far.reshape(pairs, 2, 1), (pairs, 2, LANES))
    tok = pl.BlockSpec((1, l, LANES), lambda i, j: (i, 0, j))
    tile = pl.BlockSpec((2, MOBA_BLOCK, MOBA_BLOCK), lambda i, j: (j, 0, 0))
    return pl.pallas_call(
        functools.partial(_moba_kernel, n_blocks=n_blocks),
        grid=(b, pairs),
        in_specs=[tok, tok, tok, tile, tile, pl.BlockSpec((1, 2, LANES), lambda i, j: (j, 0, 0))],
        out_specs=tok,
        out_shape=jax.ShapeDtypeStruct((b, l, w), BF16),
        scratch_shapes=[pltpu.VMEM((l, LANES), BF16)],
        compiler_params=_params("parallel", "parallel"),
        name="moba",
    )(q, k, v, own, prev, far_rows)


def _merge_kernel(x_ref, ya_ref, yb_ref, ga_ref, gb_ref, wa_ref, wb_ref, wo_ref, g2_ref, x2_ref, hn_t_ref):
    up_a = jnp.dot(ya_ref[...], wa_ref[...], preferred_element_type=F32)
    up_b = jnp.dot(yb_ref[0], wb_ref[...], preferred_element_type=F32)
    merged = ga_ref[0].astype(F32) * up_a + gb_ref[0].astype(F32) * up_b
    x2 = x_ref[0] + jnp.dot(merged.astype(BF16), wo_ref[...], preferred_element_type=F32)
    x2_ref[0] = x2
    hn_t_ref[...] = _rms(x2, g2_ref[...]).T.astype(BF16)


def _merge(x, ya_tm, yb, ga, gb, wa, wb, wo, g2):
    b, l, d = x.shape
    rows = min(IN_PROJ_ROWS, l)
    nj = l // rows
    tok = lambda width: pl.BlockSpec((1, rows, width), lambda i, j: (i, j, 0))
    full = lambda a: pl.BlockSpec(a.shape, lambda i, j: (0,) * a.ndim)
    g2r = g2.reshape(1, d)
    return pl.pallas_call(
        _merge_kernel,
        grid=(b, nj),
        in_specs=[tok(d),
                  pl.BlockSpec((rows, SSM_WIDTH), lambda i, j: (j, i)),
                  tok(ATT_WIDTH), tok(d), tok(d), full(wa), full(wb), full(wo), full(g2r)],
        out_specs=[tok(d), pl.BlockSpec((d, rows), lambda i, j: (0, i * nj + j))],
        out_shape=[jax.ShapeDtypeStruct((b, l, d), F32), jax.ShapeDtypeStruct((d, b * l), BF16)],
        compiler_params=_params("parallel", "parallel"),
        name="merge",
    )(x, ya_tm, yb, ga, gb, wa, wb, wo, g2r)


UNRANKED = float(PEER_NKEYS - 1)


def _top_rows(x, n):
    rows = []
    cur = x
    rank = jnp.full(x.shape, UNRANKED, F32)
    for r in range(n):
        mx = jnp.max(cur, axis=0, keepdims=True)
        rows.append(mx)
        hit = cur == mx
        rank = jnp.where(hit, float(r), rank)
        cur = jnp.where(hit, -jnp.inf, cur)
    return rows, rank


def _route_kernel(hn_t_ref, wq_t_ref, k1_ref, k2_ref, n1_ref, e1_ref, rank2_ref, e2_ref):
    half = PEER_QDIM // 2
    hn_t = hn_t_ref[...]
    cols = hn_t.shape[1]
    row8 = lax.broadcasted_iota(jnp.int32, (SUBLANES, cols), 0)
    for h in range(PEER_HEADS):
        q_t = jnp.dot(wq_t_ref[h * PEER_QDIM:(h + 1) * PEER_QDIM, :], hn_t, preferred_element_type=F32)
        s1 = jnp.dot(k1_ref[...], q_t[:half].astype(BF16), preferred_element_type=F32)
        s2 = jnp.dot(k2_ref[...], q_t[half:].astype(BF16), preferred_element_type=F32)
        v1, rank1 = _top_rows(s1, PEER_TOPK)
        v2, rank2 = _top_rows(s2, PEER_TOPK)
        v2_all = jnp.concatenate(v2, axis=0)
        v2_top = v2_all[:SUBLANES]
        cands = [v1[0] + v2_all, v1[1] + v2_top]
        for i in range(2, SUBLANES):
            cands.append(jnp.where(row8 < PEER_TOPK // (i + 1), v1[i] + v2_top, -jnp.inf))
        cands.append(jnp.concatenate(v1[SUBLANES:], axis=0) + v2[0])
        cand = jnp.concatenate(cands, axis=0)
        best, _ = _top_rows(cand, PEER_TOPK)
        top = best[0]
        z = functools.reduce(lambda a, b: a + b, [jnp.exp(c - top) for c in best])
        picked = jnp.where(cand >= best[-1], 1.0, 0.0)
        counts = [jnp.sum(picked[:2 * SUBLANES], axis=0, keepdims=True)]
        for i in range(1, SUBLANES):
            counts.append(jnp.sum(picked[(i + 1) * SUBLANES:(i + 2) * SUBLANES], axis=0, keepdims=True))
        tail = picked[(SUBLANES + 1) * SUBLANES:]
        counts += [tail[i:i + 1] for i in range(SUBLANES)]
        n1 = jnp.zeros_like(s1)
        for i in range(PEER_TOPK):
            n1 = jnp.where(rank1 == float(i), counts[i], n1)
        n1_ref[h] = n1
        e1_ref[h] = jnp.exp(s1 - v1[0]) / z
        rank2_ref[h] = rank2.astype(BF16)
        e2_ref[h] = jnp.exp(s2 - v2[0]).astype(BF16)


def _route(hn_t, wq_t, k1, k2):
    d, t = hn_t.shape
    cols = min(PEER_ROUTE_TOKENS, t)
    full = lambda a: pl.BlockSpec(a.shape, lambda i: (0,) * a.ndim)
    per_head = pl.BlockSpec((PEER_HEADS, PEER_NKEYS, cols), lambda i: (0, 0, i))
    shape = lambda dt: jax.ShapeDtypeStruct((PEER_HEADS, PEER_NKEYS, t), dt)
    return pl.pallas_call(
        _route_kernel,
        grid=(t // cols,),
        in_specs=[pl.BlockSpec((d, cols), lambda i: (0, i)), full(wq_t), full(k1), full(k2)],
        out_specs=[per_head, per_head, per_head, per_head],
        out_shape=[shape(F32), shape(F32), shape(BF16), shape(BF16)],
        compiler_params=_params("parallel"),
        name="peer_route",
    )(hn_t, wq_t, k1, k2)


def _peer_kernel(hn_t_ref, u_ref, vt_ref, n1_ref, e1_ref, rank2_ref, e2_ref, x2_ref, gf_ref,
                 o_ref, act0_ref, act1_ref, w0_ref, w1_ref, acc_ref, *, n_chunks, n_steps):
    g = pl.program_id(0)
    cols = hn_t_ref.shape[1]
    keys = PEER_NKEYS
    chunk_c = jnp.clip(g - 2, 0, n_steps - 1) % n_chunks

    @pl.when(g == 0)
    def _():
        for ref in (act0_ref, act1_ref, w0_ref, w1_ref, acc_ref):
            ref[...] = jnp.zeros_like(ref)

    def stages(act_a_ref, act_b_ref, w_b_ref, w_c_ref):
        def stage_a(p):
            rows = pl.ds(p * MXU_ROWS, MXU_ROWS)
            act_a_ref[rows, :] = jnp.dot(u_ref[rows, :], hn_t_ref[...], preferred_element_type=F32)

        def stage_c(p):
            rows = pl.ds(p * MXU_ROWS, MXU_ROWS)
            contrib = jnp.dot(vt_ref[rows, :], w_c_ref[...], preferred_element_type=F32)
            acc_ref[rows, :] = jnp.where(chunk_c == 0, 0.0, acc_ref[rows, :]) + contrib

        def stage_b(a_loc, lc):
            lanes = pl.ds(lc * BF16_LANES, BF16_LANES)
            rows = pl.ds(a_loc * keys, keys)
            gate = jnp.zeros((keys, BF16_LANES), BF16)
            for h in range(PEER_HEADS):
                n1a = n1_ref[h, a_loc:a_loc + 1, lanes].astype(BF16)
                e1a = e1_ref[h, a_loc:a_loc + 1, lanes].astype(BF16)
                sel = rank2_ref[h, :, lanes] < n1a
                gate = gate + jnp.where(sel, e2_ref[h, :, lanes] * e1a, jnp.zeros((), BF16))
            w_b_ref[rows, lanes] = gate * _gelu(act_b_ref[rows, lanes]).astype(BF16)

        b_blocks = [(a_loc, lc) for a_loc in range(PEER_KEYS_PER_STEP) for lc in range(cols // BF16_LANES)]
        assert u_ref.shape[0] == vt_ref.shape[0]
        mxu_pieces = [(stage, p) for p in range(u_ref.shape[0] // MXU_ROWS) for stage in (stage_a, stage_c)]
        per_piece = -(-len(b_blocks) // len(mxu_pieces))
        for i, (stage, p) in enumerate(mxu_pieces):
            stage(p)
            for blk in b_blocks[i * per_piece:(i + 1) * per_piece]:
                stage_b(*blk)

    @pl.when(g % 2 == 0)
    def _():
        stages(act0_ref, act1_ref, w1_ref, w0_ref)

    @pl.when(g % 2 == 1)
    def _():
        stages(act1_ref, act0_ref, w0_ref, w1_ref)

    @pl.when((g >= 2) & (chunk_c == n_chunks - 1))
    def _():
        x3 = x2_ref[...] + acc_ref[...].T
        o_ref[...] = _rms(x3, gf_ref[...])


def _peer(hn_t, u_bf16, vt_bf16, n1, e1, rank2, e2, x2_flat, gf):
    d, t = hn_t.shape
    n_exp = u_bf16.shape[0]
    cols = min(PEER_TOKENS, t)
    chunk = PEER_KEYS_PER_STEP * PEER_NKEYS
    n_chunks = n_exp // chunk
    n_steps = (t // cols) * n_chunks
    step_a = lambda g: jnp.minimum(g, n_steps - 1)
    step_b = lambda g: jnp.clip(g - 1, 0, n_steps - 1)
    step_c = lambda g: jnp.clip(g - 2, 0, n_steps - 1)
    per_head = pl.BlockSpec((PEER_HEADS, PEER_NKEYS, cols), lambda g: (0, 0, step_b(g) // n_chunks))
    assert PEER_KEYS_PER_STEP % SUBLANES == 0
    first_key_rows = pl.BlockSpec((PEER_HEADS, PEER_KEYS_PER_STEP, cols),
                                  lambda g: (0, step_b(g) % n_chunks, step_b(g) // n_chunks))
    gfr = gf.reshape(1, d)
    return pl.pallas_call(
        functools.partial(_peer_kernel, n_chunks=n_chunks, n_steps=n_steps),
        grid=(n_steps + 2,),
        in_specs=[pl.BlockSpec((d, cols), lambda g: (0, step_a(g) // n_chunks)),
                  pl.BlockSpec((chunk, d), lambda g: (step_a(g) % n_chunks, 0)),
                  pl.BlockSpec((d, chunk), lambda g: (0, step_c(g) % n_chunks)),
                  first_key_rows, first_key_rows, per_head, per_head,
                  pl.BlockSpec((cols, d), lambda g: (step_c(g) // n_chunks, 0)),
                  pl.BlockSpec((1, d), lambda g: (0, 0))],
        out_specs=pl.BlockSpec((cols, d), lambda g: (step_c(g) // n_chunks, 0)),
        out_shape=jax.ShapeDtypeStruct((t, d), F32),
        scratch_shapes=[pltpu.VMEM((chunk, cols), F32), pltpu.VMEM((chunk, cols), F32),
                        pltpu.VMEM((chunk, cols), BF16), pltpu.VMEM((chunk, cols), BF16),
                        pltpu.VMEM((d, cols), F32)],
        compiler_params=_params("arbitrary"),
        name="peer",
    )(hn_t, u_bf16, vt_bf16, n1, e1, rank2, e2, x2_flat, gfr)


def _layer(x, norm1_g, w_in, a_re, a_im, log_dt, b_re, b_im, c_re, c_im, d_skip, w_glu, b_glu,
           w_up_ssm, w_up_att, bias_tiles, w_out, norm2_g, w_q, k1, k2, u_emb, v_emb, out_g):
    b, l, d = x.shape
    u_tm, q, k, v, ga, gb = _in_proj(x, norm1_g, w_in.astype(BF16))
    bmat, lam_rows, cmat = _s5_matrices(a_re, a_im, log_dt, b_re, b_im, c_re, c_im)
    ya = _s5(u_tm.reshape(l * b, SSM_WIDTH), b, bmat, lam_rows, cmat, d_skip.reshape(-1),
             w_glu.astype(BF16), b_glu)
    yb = _moba(q, k, v, *bias_tiles)
    x2, hn_t = _merge(x, ya.reshape(l, b * SSM_WIDTH), yb, ga, gb, w_up_ssm.astype(BF16),
                      w_up_att.astype(BF16), w_out.astype(BF16), norm2_g)
    n1, e1, rank2, e2 = _route(hn_t, w_q.T.astype(BF16), k1.astype(BF16), k2.astype(BF16))
    out = _peer(hn_t, u_emb.astype(BF16), v_emb.T.astype(BF16), n1, e1, rank2, e2,
                x2.reshape(b * l, d), out_g)
    return out.reshape(b, l, d)


def kernel(x, norm1_g, w_in, ssm_a_re, ssm_a_im, ssm_log_dt, ssm_b_re, ssm_b_im, ssm_c_re, ssm_c_im, ssm_d, ssm_w_glu, ssm_b_glu, w_up_ssm, w_up_att, rel_bias, w_out, norm2_g, peer_w_q, peer_sub_k1, peer_sub_k2, peer_u, peer_v, final_g):
    assert w_in.shape[0] == 1, "only DEPTH == 1 is supported"
    bias_tiles = _moba_bias_tiles(rel_bias)
    l = 0
    return _layer(x, norm1_g[l], w_in[l], ssm_a_re[l], ssm_a_im[l], ssm_log_dt[l], ssm_b_re[l], ssm_b_im[l],
                  ssm_c_re[l], ssm_c_im[l], ssm_d[l], ssm_w_glu[l], ssm_b_glu[l], w_up_ssm[l], w_up_att[l],
                  bias_tiles, w_out[l], norm2_g[l], peer_w_q[l], peer_sub_k1[l], peer_sub_k2[l],
                  peer_u[l], peer_v[l], final_g)
```
